```python
import jax, jax.numpy as jnp
from jax import lax
import numpy as np


D_MODEL = 1024
BATCH = 2
SEQ = 16384
DEPTH = 2

CTX_LEN = 256
GRID_W = 64
N_HEADS = 8
QK_NOPE = 64
QK_ROPE = 32
QK_HEAD = QK_NOPE + QK_ROPE
V_HEAD = 64
Q_LORA = 384
KV_LORA = 256
KV_IN = KV_LORA + QK_ROPE
ROPE_AXIS_DIM = QK_ROPE // 2
ROPE_THETA = 10000.0
Q_BLOCK = 128
POOL_WINDOWS = (2, 4, 8, 16)
POOL_GROUPS = 4
POOL_GROUP = 128
POOL_WIDTH = POOL_GROUPS * POOL_GROUP
AB_IN = Q_LORA + KV_IN + POOL_WIDTH
MIX_OUT = N_HEADS * V_HEAD + POOL_WIDTH
D_CONV = 1024
CONV_WIDTH = 3
PEER_HEADS = 8
PEER_N_KEYS = 128
PEER_TOPK = 16
PEER_D_KEY = 128
PEER_N_EXPERTS = PEER_N_KEYS * PEER_N_KEYS
PEER_CHUNK = 128
EPS = 1e-6
N_EVEN = (DEPTH + 1) // 2
N_ODD = DEPTH // 2

kernel_name = 'hybrid_mla_pool_shortconv_peer'


def rmsnorm(x, g):
    xf = x.astype(jnp.float32)
    y = xf * lax.rsqrt(jnp.mean(xf * xf, axis=-1, keepdims=True) + EPS)
    return (y * g.astype(jnp.float32)).astype(x.dtype)


def modulate(xn, shift, scale):
    return xn * (1 + scale) + shift


def axial_rope_tables(L):
    rows = L // GRID_W
    row = jnp.broadcast_to(jnp.arange(rows, dtype=jnp.float32)[:, None], (rows, GRID_W)).reshape(-1)
    col = jnp.broadcast_to(jnp.arange(GRID_W, dtype=jnp.float32)[None, :], (rows, GRID_W)).reshape(-1)
    inv = ROPE_THETA ** (-jnp.arange(0, ROPE_AXIS_DIM, 2, dtype=jnp.float32) / ROPE_AXIS_DIM)
    ang = jnp.stack([row[:, None] * inv, col[:, None] * inv], axis=1)
    return jnp.cos(ang), jnp.sin(ang)


def apply_axial_rope(x, cos, sin):
    half = ROPE_AXIS_DIM // 2
    xs = x.reshape(x.shape[:-1] + (2, 2, half))
    x1, x2 = xs[..., 0, :], xs[..., 1, :]
    cs = cos[:, None].astype(x.dtype)
    sn = sin[:, None].astype(x.dtype)
    out = jnp.stack([x1 * cs - x2 * sn, x2 * cs + x1 * sn], axis=-2)
    return out.reshape(x.shape)


def rope_heads(t, cos, sin):
    return jnp.concatenate([t[..., :QK_NOPE], apply_axial_rope(t[..., QK_NOPE:], cos, sin)], axis=-1)


def mla_queries(h_q, q_a_norm, q_b_w, q_norm):
    B, L = h_q.shape[:2]
    q = (rmsnorm(h_q, q_a_norm) @ q_b_w).reshape(B, L, N_HEADS, QK_HEAD)
    return rmsnorm(q, q_norm)


def mla_keys_values(h_kv, kv_a_norm, kv_b_w, k_norm):
    B, L = h_kv.shape[:2]
    ckv, k_rope = h_kv[..., :KV_LORA], h_kv[..., KV_LORA:]
    kv = (rmsnorm(ckv, kv_a_norm) @ kv_b_w).reshape(B, L, N_HEADS, QK_NOPE + V_HEAD)
    k_nope, v = kv[..., :QK_NOPE], kv[..., QK_NOPE:]
    k_r = jnp.broadcast_to(k_rope[:, :, None, :], (B, L, N_HEADS, QK_ROPE))
    k = jnp.concatenate([k_nope, k_r], axis=-1)
    return rmsnorm(k, k_norm), v


def block_attention(q, k, v):
    B, L, H, Dh = q.shape
    nb = L // Q_BLOCK
    qb = q.reshape(B, nb, Q_BLOCK, H, Dh).transpose(1, 0, 2, 3, 4)
    scale = Dh ** -0.5

    def one_block(qi):
        s = jnp.einsum('bqhd,bkhd->bhqk', qi, k).astype(jnp.float32) * scale
        p = jax.nn.softmax(s, axis=-1)
        return jnp.einsum('bhqk,bkhv->bqhv', p.astype(v.dtype), v)

    o = lax.map(one_block, qb)
    return o.transpose(1, 0, 2, 3, 4).reshape(B, L, H * v.shape[-1])


def multiscale_pool(h, pool_w, pool_scale):
    B, L, _ = h.shape
    hf = h.astype(jnp.float32).reshape(B, L, POOL_GROUPS, POOL_GROUP)
    cs = jnp.concatenate([jnp.zeros((B, 1, POOL_GROUPS, POOL_GROUP), jnp.float32), jnp.cumsum(hf, axis=1)], axis=1)
    pos = jnp.arange(L)
    outs = []
    for gi, w in enumerate(POOL_WINDOWS):
        lo = jnp.clip(pos - w // 2, 0, L)
        hi = jnp.clip(pos + w - w // 2, 0, L)
        cnt = (hi - lo).astype(jnp.float32)[None, :, None]
        csg = cs[:, :, gi]
        window_sum = jnp.take(csg, hi, axis=1) - jnp.take(csg, lo, axis=1)
        outs.append(window_sum / cnt - hf[:, :, gi])
    pooled = jnp.stack(outs, axis=2).astype(h.dtype)
    y = jnp.einsum('blgc,gcd->blgd', pooled, pool_w).reshape(B, L, POOL_WIDTH)
    return y * pool_scale


def even_mixer(xm, xmc, update_ctx, cos, sin, w_in, q_a_norm, q_b_w, kv_a_norm, kv_b_w,
               q_norm, k_norm, pool_w, pool_scale, w_out):
    h = xm @ w_in
    h_q, h_kv, h_pool = h[..., :Q_LORA], h[..., Q_LORA:Q_LORA + KV_IN], h[..., Q_LORA + KV_IN:]
    k_c, v_c = mla_keys_values(xmc @ w_in[:, Q_LORA:Q_LORA + KV_IN], kv_a_norm, kv_b_w, k_norm)
    q = rope_heads(mla_queries(h_q, q_a_norm, q_b_w, q_norm), cos, sin)
    k, v = mla_keys_values(h_kv, kv_a_norm, kv_b_w, k_norm)
    k = rope_heads(k, cos, sin)
    attn = block_attention(q, jnp.concatenate([k_c, k], axis=1), jnp.concatenate([v_c, v], axis=1))
    y = jnp.concatenate([attn, multiscale_pool(h_pool, pool_w, pool_scale)], axis=-1) @ w_out
    y_c = None
    if update_ctx:
        q_c = mla_queries(xmc @ w_in[:, :Q_LORA], q_a_norm, q_b_w, q_norm)
        attn_c = block_attention(q_c, k_c, v_c)
        pool_c = multiscale_pool(xmc @ w_in[:, Q_LORA + KV_IN:], pool_w, pool_scale)
        y_c = jnp.concatenate([attn_c, pool_c], axis=-1) @ w_out
    return y, y_c


def short_conv_mixer(xm, w_in, conv_w, w_out):
    h = xm @ w_in
    bg, cg, hv = h[..., :D_CONV], h[..., D_CONV:2 * D_CONV], h[..., 2 * D_CONV:]
    u = cg * hv
    L = u.shape[1]
    up = jnp.pad(u, ((0, 0), (1, 1), (0, 0)))
    conv = up[:, 0:L] * conv_w[0] + up[:, 1:L + 1] * conv_w[1] + up[:, 2:L + 2] * conv_w[2]
    return (bg * conv) @ w_out


def peer_ffn(xm, wq, keys, u_tab, v_tab):
    B, L, D = xm.shape
    xt = xm.reshape(-1, PEER_CHUNK, D)
    half = PEER_D_KEY // 2
    n_cand = PEER_TOPK * PEER_TOPK

    def chunk(xc):
        q = (xc @ wq).reshape(PEER_CHUNK, PEER_HEADS, PEER_D_KEY)
        s1 = jnp.einsum('chd,hkd->chk', q[..., :half], keys[:, 0]).astype(jnp.float32)
        s2 = jnp.einsum('chd,hkd->chk', q[..., half:], keys[:, 1]).astype(jnp.float32)
        v1, i1 = lax.top_k(s1, PEER_TOPK)
        v2, i2 = lax.top_k(s2, PEER_TOPK)
        cand = (v1[..., :, None] + v2[..., None, :]).reshape(PEER_CHUNK, PEER_HEADS, n_cand)
        cid = (i1[..., :, None] * PEER_N_KEYS + i2[..., None, :]).reshape(PEER_CHUNK, PEER_HEADS, n_cand)
        top, pos = lax.top_k(cand, PEER_TOPK)
        eid = jnp.take_along_axis(cid, pos, axis=-1).reshape(PEER_CHUNK, PEER_HEADS * PEER_TOPK)
        g = jax.nn.softmax(top, axis=-1).reshape(PEER_CHUNK, PEER_HEADS * PEER_TOPK)
        u = jnp.take(u_tab, eid, axis=0)
        a = jax.nn.gelu(jnp.einsum('cd,ced->ce', xc, u), approximate=False)
        coef = (g * a.astype(jnp.float32)).astype(xc.dtype)
        return jnp.einsum('ce,ced->cd', coef, jnp.take(v_tab, eid, axis=0))

    return lax.map(chunk, xt).reshape(B, L, D)


def setup_inputs(seed: int = 0) -> dict:
    key = jax.random.key(seed)
    ks = jax.random.split(key, 26)
    D = D_MODEL

    def nrm(k, shape, std):
        return std * jax.random.normal(k, shape, jnp.float32)

    return {
        'x': nrm(ks[0], (BATCH, SEQ, D), 1.0),
        'c': nrm(ks[1], (BATCH, D), 1.0),
        'ctx': nrm(ks[2], (BATCH, CTX_LEN, D), 1.0),
        'c_ctx': nrm(ks[3], (D,), 1.0),
        'mod_w': nrm(ks[4], (DEPTH, D, 6 * D), 0.5 * D ** -0.5),
        'mod_b': nrm(ks[5], (DEPTH, 6 * D), 0.02),
        'norm_mix': 1.0 + nrm(ks[6], (DEPTH, D), 0.02),
        'norm_ffn': 1.0 + nrm(ks[7], (DEPTH, D), 0.02),
        'ab_w_in': nrm(ks[8], (N_EVEN, D, AB_IN), D ** -0.5),
        'q_a_norm': 1.0 + nrm(ks[9], (N_EVEN, Q_LORA), 0.02),
        'q_b_w': nrm(ks[10], (N_EVEN, Q_LORA, N_HEADS * QK_HEAD), Q_LORA ** -0.5),
        'kv_a_norm': 1.0 + nrm(ks[11], (N_EVEN, KV_LORA), 0.02),
        'kv_b_w': nrm(ks[12], (N_EVEN, KV_LORA, N_HEADS * (QK_NOPE + V_HEAD)), KV_LORA ** -0.5),
        'q_norm': 1.0 + nrm(ks[13], (N_EVEN, QK_HEAD), 0.02),
        'k_norm': 1.0 + nrm(ks[14], (N_EVEN, QK_HEAD), 0.02),
        'pool_w': nrm(ks[15], (N_EVEN, POOL_GROUPS, POOL_GROUP, POOL_GROUP), POOL_GROUP ** -0.5),
        'pool_scale': 1.0 + nrm(ks[16], (N_EVEN, POOL_WIDTH), 0.1),
        'ab_w_out': nrm(ks[17], (N_EVEN, MIX_OUT, D), MIX_OUT ** -0.5),
        'c_w_in': nrm(ks[18], (N_ODD, D, 3 * D_CONV), D ** -0.5),
        'c_conv': nrm(ks[19], (N_ODD, CONV_WIDTH, D_CONV), CONV_WIDTH ** -0.5),
        'c_w_out': nrm(ks[20], (N_ODD, D_CONV, D), D_CONV ** -0.5),
        'peer_wq': nrm(ks[21], (DEPTH, D, PEER_HEADS * PEER_D_KEY), D ** -0.5),
        'peer_keys': nrm(ks[22], (DEPTH, PEER_HEADS, 2, PEER_N_KEYS, PEER_D_KEY // 2), (PEER_D_KEY // 2) ** -0.5),
        'peer_u': nrm(ks[23], (DEPTH, PEER_N_EXPERTS, D), D ** -0.5),
        'peer_v': nrm(ks[24], (DEPTH, PEER_N_EXPERTS, D), 1.0),
    }


def reference(x, c, ctx, c_ctx, mod_w, mod_b, norm_mix, norm_ffn, ab_w_in, q_a_norm, q_b_w,
              kv_a_norm, kv_b_w, q_norm, k_norm, pool_w, pool_scale, ab_w_out, c_w_in, c_conv,
              c_w_out, peer_wq, peer_keys, peer_u, peer_v):
    B, L, D = x.shape
    cos, sin = axial_rope_tables(L)
    s_c = jax.nn.silu(c)
    s_cc = jax.nn.silu(c_ctx)
    ctx_s = ctx
    for i in range(DEPTH):
        even = (i % 2 == 0)
        update_ctx = any(j % 2 == 0 for j in range(i + 1, DEPTH))
        need_ctx_in = even or update_ctx
        mod = (s_c @ mod_w[i] + mod_b[i]).reshape(B, 6, D).transpose(1, 0, 2)[:, :, None, :]
        xm = modulate(rmsnorm(x, norm_mix[i]), mod[0], mod[1])
        if need_ctx_in:
            mod_c = (s_cc @ mod_w[i] + mod_b[i]).reshape(6, 1, 1, D)
            xmc = modulate(rmsnorm(ctx_s, norm_mix[i]), mod_c[0], mod_c[1])
        if even:
            e = i // 2
            y, y_c = even_mixer(xm, xmc, update_ctx, cos, sin, ab_w_in[e], q_a_norm[e], q_b_w[e],
                                kv_a_norm[e], kv_b_w[e], q_norm[e], k_norm[e], pool_w[e],
                                pool_scale[e], ab_w_out[e])
        else:
            o = i // 2
            y = short_conv_mixer(xm, c_w_in[o], c_conv[o], c_w_out[o])
            y_c = short_conv_mixer(xmc, c_w_in[o], c_conv[o], c_w_out[o]) if update_ctx else None
        x = x + mod[2] * y
        xf = modulate(rmsnorm(x, norm_ffn[i]), mod[3], mod[4])
        x = x + mod[5] * peer_ffn(xf, peer_wq[i], peer_keys[i], peer_u[i], peer_v[i])
        if update_ctx:
            ctx_s = ctx_s + mod_c[2] * y_c
            xfc = modulate(rmsnorm(ctx_s, norm_ffn[i]), mod_c[3], mod_c[4])
            ctx_s = ctx_s + mod_c[5] * peer_ffn(xfc, peer_wq[i], peer_keys[i], peer_u[i], peer_v[i])
    return x
```

```python
import functools

import jax
import jax.numpy as jnp
from jax import lax
from jax.experimental import pallas as pl
from jax.experimental.pallas import tpu as pltpu

F32 = jnp.float32
BF16 = jnp.bfloat16

EPS = 1e-6
GRID_W = 64
N_HEADS = 8
QK_NOPE = 64
QK_ROPE = 32
QK_HEAD = QK_NOPE + QK_ROPE
V_HEAD = 64
Q_LORA = 384
KV_LORA = 256
ROPE_AXIS_DIM = QK_ROPE // 2
ROPE_THETA = 10000.0
POOL_WINDOWS = (2, 4, 8, 16)
POOL_GROUP = 128
POOL_WIDTH = 512
PEER_HEADS = 8
PEER_N_KEYS = 128
PEER_TOPK = 16
PEER_HALF = 64

LANES = 128
HP = LANES
POOL_HALO = 16
CONV_HALO = 8
NEG = -jnp.inf
NOT_SELECTED = 64.0
VMEM_LIMIT = 56 * 1024 * 1024


def _cp(*sem):
    return pltpu.CompilerParams(dimension_semantics=sem, vmem_limit_bytes=VMEM_LIMIT)


def _rms(x, g, n):
    ms = jnp.sum(x * x, axis=-1, keepdims=True) * (1.0 / n)
    return x * lax.rsqrt(ms + EPS) * g


def _modulated_norm(x, g, shift, scale):
    return _rms(x, g, x.shape[-1]) * (1.0 + scale) + shift


def _dot(a, b):
    return jnp.dot(a, b, preferred_element_type=F32)


def _mod_kernel(c_ref, w_ref, b_ref, o_ref):
    c = c_ref[...]
    s = c * jax.nn.sigmoid(c)
    o_ref[0] = _dot(s.astype(BF16), w_ref[0].astype(BF16)) + b_ref[0]


def _mod_vectors(cvec, mod_w, mod_b):
    depth, d, n = mod_w.shape
    tn = 1536
    return pl.pallas_call(
        _mod_kernel,
        grid=(depth, n // tn),
        in_specs=[
            pl.BlockSpec((8, d), lambda i, j: (0, 0)),
            pl.BlockSpec((1, d, tn), lambda i, j: (i, 0, j)),
            pl.BlockSpec((1, 1, tn), lambda i, j: (i, 0, j)),
        ],
        out_specs=pl.BlockSpec((1, 8, tn), lambda i, j: (i, 0, j)),
        out_shape=jax.ShapeDtypeStruct((depth, 8, n), F32),
        compiler_params=_cp("parallel", "parallel"),
        name="mod_vectors",
    )(cvec, mod_w, mod_b.reshape(depth, 1, n))


def _rope(xh, c, sa, sb):
    return xh * c + pltpu.roll(xh, LANES - 8, 1) * sa + pltpu.roll(xh, 8, 1) * sb


def _front0_kernel(x_ref, mod_ref, g_ref, win_ref, qan_ref, qbw_ref, kvan_ref, kvbw_ref, qn_ref, kn_ref,
                   c_ref, sa_ref, sb_ref, q_ref, k_ref, kv_ref, hp_ref):
    x = x_ref[0]
    mod = mod_ref[0]
    xm = _modulated_norm(x, g_ref[...], mod[0:1], mod[1:2])
    h = _dot(xm.astype(BF16), win_ref[...])
    o1, o2, o3 = Q_LORA, Q_LORA + KV_LORA, Q_LORA + KV_LORA + HP
    hq = _rms(h[:, :o1], qan_ref[...], Q_LORA)
    ckv = _rms(h[:, o1:o2], kvan_ref[...], KV_LORA)
    krope = h[:, o2:o3]
    hp_ref[0] = h[:, o3:]
    qf = _dot(hq.astype(BF16), qbw_ref[...])
    kvf = _dot(ckv.astype(BF16), kvbw_ref[...])
    c, sa, sb = c_ref[...], sa_ref[...], sb_ref[...]
    lane = lax.broadcasted_iota(jnp.int32, krope.shape, 1)
    qscale = QK_HEAD ** -0.5
    for hh in range(N_HEADS):
        sl = slice(hh * HP, (hh + 1) * HP)
        qh = _rope(_rms(qf[:, sl], qn_ref[...], QK_HEAD), c, sa, sb)
        q_ref[0, hh] = (qh * qscale).astype(BF16)
        kvh = kvf[:, sl]
        kh = jnp.where(lane < QK_NOPE, kvh, krope)
        kh = _rope(_rms(kh, kn_ref[...], QK_HEAD), c, sa, sb)
        k_ref[0, hh] = kh.astype(BF16)
        kv_ref[0, hh] = kvh.astype(BF16)


def _front0(x, mod6, gmix, w, tabs, tm):
    b, l, d = x.shape
    c, sa, sb = tabs
    nw = w["win"].shape[1]
    full = lambda a: pl.BlockSpec(a.shape, lambda bi, i: (0,) * a.ndim)
    hshape = jax.ShapeDtypeStruct((b, N_HEADS, l, HP), BF16)
    hspec = pl.BlockSpec((1, N_HEADS, tm, HP), lambda bi, i: (bi, 0, i, 0))
    tspec = pl.BlockSpec((tm, HP), lambda bi, i: (i, 0))
    return pl.pallas_call(
        _front0_kernel,
        grid=(b, l // tm),
        in_specs=[
            pl.BlockSpec((1, tm, d), lambda bi, i: (bi, i, 0)),
            pl.BlockSpec((1, 6, d), lambda bi, i: (bi, 0, 0)),
            full(gmix), full(w["win"]), full(w["qan"]), full(w["qbw"]), full(w["kvan"]), full(w["kvbw"]),
            full(w["qn"]), full(w["kn"]), tspec, tspec, tspec,
        ],
        out_specs=[hspec, hspec, hspec, pl.BlockSpec((1, tm, POOL_WIDTH), lambda bi, i: (bi, i, 0))],
        out_shape=[hshape, hshape, hshape, jax.ShapeDtypeStruct((b, l, POOL_WIDTH), F32)],
        compiler_params=_cp("parallel", "parallel"),
        name="front0",
    )(x, mod6, gmix, w["win"], w["qan"], w["qbw"], w["kvan"], w["kvbw"], w["qn"], w["kn"], c, sa, sb)


def _attn_kernel(q_ref, kc_ref, vc_ref, k_ref, v_ref, o_ref, *, tk, nk):
    q = q_ref[0, 0]
    tq = q.shape[0]

    def step(kc, vc, carry):
        m, l, acc = carry
        s = lax.dot_general(q, kc, (((1,), (1,)), ((), ())), preferred_element_type=F32)
        m_new = jnp.maximum(m, jnp.max(s, axis=1, keepdims=True))
        alpha = jnp.exp(m - m_new)
        p = jnp.exp(s - m_new)
        l = alpha * l + jnp.sum(p, axis=1, keepdims=True)
        acc = alpha * acc + _dot(p.astype(BF16), vc)
        return m_new, l, acc

    init = (jnp.full((tq, 1), NEG, F32), jnp.zeros((tq, 1), F32), jnp.zeros((tq, HP), F32))
    carry = step(kc_ref[0, 0], vc_ref[0, 0], init)

    def body(i, carry):
        off = pl.multiple_of(i * tk, tk)
        return step(k_ref[0, 0, pl.ds(off, tk), :], v_ref[0, 0, pl.ds(off, tk), :], carry)

    _, l, acc = lax.fori_loop(0, nk, body, carry)
    o_ref[0] = (acc / l).astype(BF16)


def _attention(q, kc, vc, k, v, tq, tk):
    b, nh, l, _ = q.shape
    nc = kc.shape[2]
    return pl.pallas_call(
        functools.partial(_attn_kernel, tk=tk, nk=l // tk),
        grid=(b, nh, l // tq),
        in_specs=[
            pl.BlockSpec((1, 1, tq, HP), lambda bi, h, i: (bi, h, i, 0)),
            pl.BlockSpec((1, 1, nc, HP), lambda bi, h, i: (bi, h, 0, 0)),
            pl.BlockSpec((1, 1, nc, HP), lambda bi, h, i: (bi, h, 0, 0)),
            pl.BlockSpec((1, 1, l, HP), lambda bi, h, i: (bi, h, 0, 0)),
            pl.BlockSpec((1, 1, l, HP), lambda bi, h, i: (bi, h, 0, 0)),
        ],
        out_specs=pl.BlockSpec((1, tq, HP), lambda bi, h, i: (bi, i, h)),
        out_shape=jax.ShapeDtypeStruct((b, l, nh * HP), BF16),
        compiler_params=_cp("parallel", "parallel", "arbitrary"),
        name="attention",
    )(q, kc, vc, k, v)


def _mix0_kernel(o_ref, hp_ref, hprev_ref, hnext_ref, x_ref, mod_ref, pw_ref, ps_ref, woa_ref, wop_ref,
                 x1_ref, *, tm, seq):
    i = pl.program_id(1)
    nblk = pl.num_programs(1)
    cur = hp_ref[0]
    prev = jnp.where(i > 0, hprev_ref[0], 0.0)
    nxt = jnp.where(i < nblk - 1, hnext_ref[0], 0.0)
    ext = jnp.concatenate([prev, cur, nxt], axis=0)
    n = tm + 2 * POOL_HALO
    row = i * tm + lax.broadcasted_iota(jnp.int32, (tm, 1), 0)
    ys = []
    s = ext
    width = 1
    for g, w in enumerate(POOL_WINDOWS):
        while width < w:
            s = s + pltpu.roll(s, n - width, 0)
            width *= 2
        sl = slice(g * POOL_GROUP, (g + 1) * POOL_GROUP)
        win = pltpu.roll(s[:, sl], w // 2, 0)[POOL_HALO:POOL_HALO + tm]
        lo = jnp.maximum(row - w // 2, 0)
        hi = jnp.minimum(row + (w - w // 2), seq)
        cnt = (hi - lo).astype(F32)
        pooled = win / cnt - cur[:, sl]
        ys.append(_dot(pooled.astype(BF16), pw_ref[g]))
    y = jnp.concatenate(ys, axis=1) * ps_ref[...]
    out = _dot(o_ref[0], woa_ref[...]) + _dot(y.astype(BF16), wop_ref[...])
    x1_ref[0] = x_ref[0] + mod_ref[0][2:3] * out


def _mix0(o, hp, x, mod6, w, tm):
    b, l, d = x.shape
    r = tm // POOL_HALO
    nh = l // POOL_HALO
    full = lambda a: pl.BlockSpec(a.shape, lambda bi, i: (0,) * a.ndim)
    return pl.pallas_call(
        functools.partial(_mix0_kernel, tm=tm, seq=l),
        grid=(b, l // tm),
        in_specs=[
            pl.BlockSpec((1, tm, o.shape[2]), lambda bi, i: (bi, i, 0)),
            pl.BlockSpec((1, tm, POOL_WIDTH), lambda bi, i: (bi, i, 0)),
            pl.BlockSpec((1, POOL_HALO, POOL_WIDTH), lambda bi, i: (bi, jnp.maximum(i * r - 1, 0), 0)),
            pl.BlockSpec((1, POOL_HALO, POOL_WIDTH), lambda bi, i: (bi, jnp.minimum((i + 1) * r, nh - 1), 0)),
            pl.BlockSpec((1, tm, d), lambda bi, i: (bi, i, 0)),
            pl.BlockSpec((1, 6, d), lambda bi, i: (bi, 0, 0)),
            full(w["pw"]), full(w["ps"]), full(w["woa"]), full(w["wop"]),
        ],
        out_specs=pl.BlockSpec((1, tm, d), lambda bi, i: (bi, i, 0)),
        out_shape=jax.ShapeDtypeStruct((b, l, d), F32),
        compiler_params=_cp("parallel", "parallel"),
        name="mix0",
    )(o, hp, hp, hp, x, mod6, w["pw"], w["ps"], w["woa"], w["wop"])


def _conv_kernel(x_ref, xprev_ref, xnext_ref, mod_ref, g_ref, win_ref, cw_ref, wout_ref, o_ref, *, tm):
    i = pl.program_id(1)
    nblk = pl.num_programs(1)
    x = x_ref[0]
    mod = mod_ref[0]
    xe = jnp.concatenate([xprev_ref[0], x, xnext_ref[0]], axis=0)
    n = tm + 2 * CONV_HALO
    xm = _modulated_norm(xe, g_ref[...], mod[0:1], mod[1:2]).astype(BF16)
    d = x.shape[1]
    bg = _dot(xm[CONV_HALO:CONV_HALO + tm], win_ref[:, :d])
    ch = _dot(xm, win_ref[:, d:])
    u = ch[:, :d] * ch[:, d:]
    r = lax.broadcasted_iota(jnp.int32, (n, 1), 0)
    inside = jnp.logical_and(jnp.logical_or(i > 0, r >= CONV_HALO),
                             jnp.logical_or(i < nblk - 1, r < CONV_HALO + tm))
    u = jnp.where(inside, u, 0.0)
    cw = cw_ref[...]
    conv = pltpu.roll(u, 1, 0) * cw[0:1] + u * cw[1:2] + pltpu.roll(u, n - 1, 0) * cw[2:3]
    y = _dot((bg * conv[CONV_HALO:CONV_HALO + tm]).astype(BF16), wout_ref[...])
    o_ref[0] = x + mod[2:3] * y


def _conv_mixer(x, mod6, gmix, w, tm):
    b, l, d = x.shape
    r = tm // CONV_HALO
    nh = l // CONV_HALO
    full = lambda a: pl.BlockSpec(a.shape, lambda bi, i: (0,) * a.ndim)
    return pl.pallas_call(
        functools.partial(_conv_kernel, tm=tm),
        grid=(b, l // tm),
        in_specs=[
            pl.BlockSpec((1, tm, d), lambda bi, i: (bi, i, 0)),
            pl.BlockSpec((1, CONV_HALO, d), lambda bi, i: (bi, jnp.maximum(i * r - 1, 0), 0)),
            pl.BlockSpec((1, CONV_HALO, d), lambda bi, i: (bi, jnp.minimum((i + 1) * r, nh - 1), 0)),
            pl.BlockSpec((1, 6, d), lambda bi, i: (bi, 0, 0)),
            full(gmix), full(w["win"]), full(w["cw"]), full(w["wout"]),
        ],
        out_specs=pl.BlockSpec((1, tm, d), lambda bi, i: (bi, i, 0)),
        out_shape=jax.ShapeDtypeStruct((b, l, d), F32),
        compiler_params=_cp("parallel", "parallel"),
        name="conv_mixer",
    )(x, x, x, mod6, gmix, w["win"], w["cw"], w["wout"])


def _top16(s, iota):
    rank = jnp.full(s.shape, NOT_SELECTED, F32)
    riota = lax.broadcasted_iota(jnp.int32, (PEER_TOPK, s.shape[1]), 0)
    vals = jnp.zeros((PEER_TOPK, s.shape[1]), F32)
    for it in range(PEER_TOPK):
        m = jnp.max(s, axis=0, keepdims=True)
        idx = jnp.min(jnp.where(s == m, iota, float(PEER_N_KEYS)), axis=0, keepdims=True)
        sel = iota == idx
        rank = jnp.where(sel, float(it), rank)
        s = jnp.where(sel, NEG, s)
        vals = jnp.where(riota == it, m, vals)
    return vals, rank


def _pair_counts(v1, v2):
    t = v1.shape[1]
    half = PEER_TOPK // 2
    blocks = [v1[0:1] + v2]
    pos = [lax.broadcasted_iota(jnp.int32, (PEER_TOPK, t), 0).astype(F32)]
    for a in range(1, PEER_TOPK):
        blocks.append(v1[a:a + 1] + v2[:half])
        pos.append(lax.broadcasted_iota(jnp.int32, (half, t), 0).astype(F32) + float(a * PEER_TOPK))
    cand = jnp.concatenate(blocks, axis=0)
    pos = jnp.concatenate(pos, axis=0)
    chosen = jnp.zeros(cand.shape, F32)
    for _ in range(PEER_TOPK):
        m = jnp.max(cand, axis=0, keepdims=True)
        pmin = jnp.min(jnp.where(cand == m, pos, 1e9), axis=0, keepdims=True)
        sel = pos == pmin
        chosen = jnp.where(sel, 1.0, chosen)
        cand = jnp.where(sel, NEG, cand)
    e1 = jnp.exp(v1 - v1[0:1])
    e2 = jnp.exp(v2 - v2[0:1])
    riota = lax.broadcasted_iota(jnp.int32, (PEER_TOPK, t), 0)
    counts = jnp.zeros((PEER_TOPK, t), F32)
    z = jnp.zeros((1, t), F32)
    off = 0
    for a in range(PEER_TOPK):
        nb = PEER_TOPK if a == 0 else half
        ch = chosen[off:off + nb]
        off += nb
        counts = jnp.where(riota == a, jnp.sum(ch, axis=0, keepdims=True), counts)
        z = z + e1[a:a + 1] * jnp.sum(ch * e2[:nb], axis=0, keepdims=True)
    return counts, z


def _route_kernel(x_ref, mod_ref, g_ref, wqt_ref, k1_ref, k2_ref, xft_ref, r2_ref, e2_ref, n1_ref, e1_ref,
                  qt_s, s1_s, s2_s, *, tm):
    x = x_ref[0]
    mod = mod_ref[0]
    xf = _modulated_norm(x, g_ref[...], mod[3:4], mod[4:5])
    xft = xf.T.astype(BF16)
    xft_ref[0] = xft
    qt_s[...] = _dot(wqt_ref[...], xft).astype(BF16)
    nc = tm // LANES
    iota = lax.broadcasted_iota(jnp.int32, (PEER_N_KEYS, LANES), 0).astype(F32)

    def head(h, _):
        qh = qt_s[pl.ds(pl.multiple_of(h * HP, HP), HP), :]
        s1 = _dot(k1_ref[h], qh)
        s2 = _dot(k2_ref[h], qh)
        for c in range(nc):
            s1_s[c] = s1[:, c * LANES:(c + 1) * LANES]
            s2_s[c] = s2[:, c * LANES:(c + 1) * LANES]

        def chunk(c, _):
            a1 = s1_s[c]
            a2 = s2_s[c]
            v1, r1 = _top16(a1, iota)
            v2, r2 = _top16(a2, iota)
            counts, z = _pair_counts(v1, v2)
            n1 = jnp.zeros(a1.shape, F32)
            for a in range(PEER_TOPK):
                n1 = jnp.where(r1 == float(a), counts[a:a + 1], n1)
            r2_ref[0, h, c] = r2.astype(BF16)
            e2_ref[0, h, c] = jnp.exp(a2 - v2[0:1]).astype(BF16)
            n1_ref[0, h, c] = n1
            e1_ref[0, h, c] = jnp.exp(a1 - v1[0:1]) / z
            return 0

        lax.fori_loop(0, nc, chunk, 0)
        return 0

    lax.fori_loop(0, PEER_HEADS, head, 0)


def _route(x, mod6, gffn, w, tm):
    b, l, d = x.shape
    nc = tm // LANES
    full = lambda a: pl.BlockSpec(a.shape, lambda bi, i: (0,) * a.ndim)
    rshape = lambda dt: jax.ShapeDtypeStruct((b, PEER_HEADS, l // LANES, PEER_N_KEYS, LANES), dt)
    rspec = pl.BlockSpec((1, PEER_HEADS, nc, PEER_N_KEYS, LANES), lambda bi, i: (bi, 0, i, 0, 0))
    return pl.pallas_call(
        functools.partial(_route_kernel, tm=tm),
        grid=(b, l // tm),
        in_specs=[
            pl.BlockSpec((1, tm, d), lambda bi, i: (bi, i, 0)),
            pl.BlockSpec((1, 6, d), lambda bi, i: (bi, 0, 0)),
            full(gffn), full(w["wqt"]), full(w["k1"]), full(w["k2"]),
        ],
        out_specs=[pl.BlockSpec((1, d, tm), lambda bi, i: (bi, 0, i)), rspec, rspec, rspec, rspec],
        out_shape=[jax.ShapeDtypeStruct((b, d, l), BF16), rshape(BF16), rshape(BF16), rshape(F32), rshape(F32)],
        scratch_shapes=[
            pltpu.VMEM((PEER_HEADS * HP, tm), BF16),
            pltpu.VMEM((nc, PEER_N_KEYS, LANES), F32),
            pltpu.VMEM((nc, PEER_N_KEYS, LANES), F32),
        ],
        compiler_params=_cp("parallel", "parallel"),
        name="peer_route",
    )(x, mod6, gffn, w["wqt"], w["k1"], w["k2"])


def _expert_kernel(xft_ref, u_ref, vt_ref, r2_ref, e2_ref, n1_ref, e1_ref, x_ref, mod_ref, o_ref,
                   acc_s, wa_s, *, tm, te):
    e = pl.program_id(2)

    @pl.when(e == 0)
    def _():
        acc_s[...] = jnp.zeros_like(acc_s)

    at = _dot(u_ref[...], xft_ref[0])
    ga = at * (1.0 + lax.erf(at * (2.0 ** -0.5)))
    nc = tm // LANES
    for s in range(te // PEER_N_KEYS):
        for c in range(nc):
            w = jnp.zeros((PEER_N_KEYS, LANES), F32)
            for h in range(PEER_HEADS):
                n = n1_ref[0, h, c, s:s + 1, :]
                e1 = e1_ref[0, h, c, s:s + 1, :]
                r2 = r2_ref[0, h, c].astype(F32)
                e2 = e2_ref[0, h, c].astype(F32)
                w = w + jnp.where(r2 < n, e2 * e1, 0.0)
            blk = w * ga[s * PEER_N_KEYS:(s + 1) * PEER_N_KEYS, c * LANES:(c + 1) * LANES]
            wa_s[s * PEER_N_KEYS:(s + 1) * PEER_N_KEYS, c * LANES:(c + 1) * LANES] = blk.astype(BF16)
    acc_s[...] += _dot(vt_ref[...], wa_s[...])

    @pl.when(e == pl.num_programs(2) - 1)
    def _():
        o_ref[0] = x_ref[0] + (0.5 * mod_ref[0][5:6]) * acc_s[...].T


def _experts(xft, u, vt, r2, e2, n1, e1, x, mod6, tm, te):
    b, l, d = x.shape
    ne = u.shape[0]
    nc = tm // LANES
    rspec = pl.BlockSpec((1, PEER_HEADS, nc, PEER_N_KEYS, LANES), lambda bi, i, e: (bi, 0, i, 0, 0))
    nspec = pl.BlockSpec((1, PEER_HEADS, nc, te // PEER_N_KEYS, LANES), lambda bi, i, e: (bi, 0, i, e, 0))
    return pl.pallas_call(
        functools.partial(_expert_kernel, tm=tm, te=te),
        grid=(b, l // tm, ne // te),
        in_specs=[
            pl.BlockSpec((1, d, tm), lambda bi, i, e: (bi, 0, i)),
            pl.BlockSpec((te, d), lambda bi, i, e: (e, 0)),
            pl.BlockSpec((d, te), lambda bi, i, e: (0, e)),
            rspec, rspec, nspec, nspec,
            pl.BlockSpec((1, tm, d), lambda bi, i, e: (bi, i, 0)),
            pl.BlockSpec((1, 6, d), lambda bi, i, e: (bi, 0, 0)),
        ],
        out_specs=pl.BlockSpec((1, tm, d), lambda bi, i, e: (bi, i, 0)),
        out_shape=jax.ShapeDtypeStruct((b, l, d), F32),
        scratch_shapes=[pltpu.VMEM((d, tm), F32), pltpu.VMEM((te, tm), BF16)],
        compiler_params=_cp("parallel", "parallel", "arbitrary"),
        name="peer_experts",
    )(xft, u, vt, r2, e2, n1, e1, x, mod6)


def _rope_tables(l):
    rows = l // GRID_W
    row = jnp.broadcast_to(jnp.arange(rows, dtype=F32)[:, None], (rows, GRID_W)).reshape(-1)
    col = jnp.broadcast_to(jnp.arange(GRID_W, dtype=F32)[None, :], (rows, GRID_W)).reshape(-1)
    inv = ROPE_THETA ** (-jnp.arange(0, ROPE_AXIS_DIM, 2, dtype=F32) / ROPE_AXIS_DIM)
    half = ROPE_AXIS_DIM // 2
    ang = jnp.stack([row[:, None] * inv, col[:, None] * inv], axis=1)
    cos, sin = jnp.cos(ang), jnp.sin(ang)
    zeros = jnp.zeros_like(sin)
    c32 = jnp.concatenate([cos, cos], axis=2).reshape(l, QK_ROPE)
    sa32 = jnp.concatenate([-sin, zeros], axis=2).reshape(l, QK_ROPE)
    sb32 = jnp.concatenate([zeros, sin], axis=2).reshape(l, QK_ROPE)
    del half
    pad = lambda t, fill: jnp.concatenate(
        [jnp.full((l, QK_NOPE), fill, F32), t, jnp.full((l, HP - QK_HEAD), fill, F32)], axis=1)
    return pad(c32, 1.0), pad(sa32, 0.0), pad(sb32, 0.0)


def _identity_tables(n):
    return jnp.ones((n, HP), F32), jnp.zeros((n, HP), F32), jnp.zeros((n, HP), F32)


def _pad_heads(w, dh):
    lead = w.shape[:-1]
    w = w.reshape(lead + (N_HEADS, dh))
    w = jnp.pad(w, [(0, 0)] * len(lead) + [(0, 0), (0, HP - dh)])
    return w.reshape(lead + (N_HEADS * HP,))


def _peer_weights(wq, keys, u_tab, v_tab):
    k1 = jnp.pad(keys[:, 0], ((0, 0), (0, 0), (0, PEER_HALF)))
    k2 = jnp.pad(keys[:, 1], ((0, 0), (0, 0), (PEER_HALF, 0)))
    return {"wqt": wq.T.astype(BF16), "k1": k1.astype(BF16), "k2": k2.astype(BF16),
            "u": u_tab.astype(BF16), "vt": v_tab.T.astype(BF16)}


def _peer(x, mod6, gffn, w, tm, te):
    xft, r2, e2, n1, e1 = _route(x, mod6, gffn, w, tm)
    return _experts(xft, w["u"], w["vt"], r2, e2, n1, e1, x, mod6, tm, te)


def kernel(x, c, ctx, c_ctx, mod_w, mod_b, norm_mix, norm_ffn, ab_w_in, q_a_norm, q_b_w, kv_a_norm, kv_b_w,
           q_norm, k_norm, pool_w, pool_scale, ab_w_out, c_w_in, c_conv, c_w_out, peer_wq, peer_keys, peer_u,
           peer_v):
    b, l, d = x.shape
    nctx = ctx.shape[1]
    depth = mod_w.shape[0]
    tm = 512 if l % 512 == 0 else 256
    tq = tm
    tk = 1024 if l % 1024 == 0 else 512
    te = 1024

    cvec = jnp.zeros((8, d), F32).at[:b].set(c).at[b].set(c_ctx)
    mods = _mod_vectors(cvec, mod_w, mod_b).reshape(depth, 8, 6, d)

    for i in range(depth):
        mod6 = mods[i, :b]
        if i % 2 == 0:
            e = i // 2
            w_in = ab_w_in[e]
            o1, o2, o3 = Q_LORA, Q_LORA + KV_LORA, Q_LORA + KV_LORA + QK_ROPE
            krope = jnp.pad(w_in[:, o2:o3], ((0, 0), (QK_NOPE, HP - QK_HEAD)))
            w0 = {
                "win": jnp.concatenate([w_in[:, :o2], krope, w_in[:, o3:]], axis=1).astype(BF16),
                "qan": q_a_norm[e][None], "kvan": kv_a_norm[e][None],
                "qbw": _pad_heads(q_b_w[e], QK_HEAD).astype(BF16),
                "kvbw": kv_b_w[e].astype(BF16),
                "qn": jnp.pad(q_norm[e], (0, HP - QK_HEAD))[None],
                "kn": jnp.pad(k_norm[e], (0, HP - QK_HEAD))[None],
            }
            gmix = norm_mix[i][None]
            q, k, kv, hp = _front0(x, mod6, gmix, w0, _rope_tables(l), tm)
            mod6c = jnp.broadcast_to(mods[i, b][None], (b, 6, d))
            _, kc, kvc, _ = _front0(ctx, mod6c, gmix, w0, _identity_tables(nctx), nctx)
            o = _attention(q, kc, kvc, k, kv, tq, tk)
            w_out = ab_w_out[e]
            nv = N_HEADS * V_HEAD
            woa = jnp.pad(w_out[:nv].reshape(N_HEADS, V_HEAD, d), ((0, 0), (HP - V_HEAD, 0), (0, 0)))
            wm = {"pw": pool_w[e].astype(BF16), "ps": pool_scale[e][None],
                  "woa": woa.reshape(N_HEADS * HP, d).astype(BF16), "wop": w_out[nv:].astype(BF16)}
            x = _mix0(o, hp, x, mod6, wm, tm)
        else:
            o = i // 2
            wc = {"win": c_w_in[o].astype(BF16), "cw": c_conv[o], "wout": c_w_out[o].astype(BF16)}
            x = _conv_mixer(x, mod6, norm_mix[i][None], wc, tm)
        wp = _peer_weights(peer_wq[i], peer_keys[i], peer_u[i], peer_v[i])
        x = _peer(x, mod6, norm_ffn[i][None], wp, tm, te)
    return x
```
